```python
import math, functools
import jax, jax.numpy as jnp
from jax import lax
import numpy as np

D_MODEL = 2048
BATCH = 2
SEQ = 4096
DEPTH = 1
DEC_BATCH = 128
DEC_SEQ = 4
PAST_LEN = 2048
PAGE_SIZE = 128

N_META = 16
N_HEADS = 16
HEAD_DIM = 128
N_KV_HEADS = 4
GROUP = N_HEADS // N_KV_HEADS
D_ATTN = N_HEADS * HEAD_DIM
D_KV = N_KV_HEADS * HEAD_DIM
N_IDX_HEADS = 16
IDX_DIM = 64
TOPK_MAX = 256
D_CONV = D_MODEL
CONV_W = 3
D_FF = 5632
Q_BLOCK = 128
LN_EPS = 1e-5
ALPHA = (2.0 * DEPTH) ** 0.25
BETA = (8.0 * DEPTH) ** -0.25
PROJ_SIZES = (D_CONV, D_CONV, D_CONV, D_ATTN, D_KV, D_KV, N_IDX_HEADS * IDX_DIM, IDX_DIM, N_IDX_HEADS, D_MODEL, D_MODEL)
V_COL_INDEX = 5
D_IN = sum(PROJ_SIZES)

kernel_name = "hybrid_shortconv_dsa_macaron_deepnorm_step"


def layer_norm(x, g, b):
    xf = x.astype(jnp.float32)
    mu = jnp.mean(xf, axis=-1, keepdims=True)
    var = jnp.mean(jnp.square(xf - mu), axis=-1, keepdims=True)
    y = (xf - mu) * lax.rsqrt(var + LN_EPS) * g.astype(jnp.float32) + b.astype(jnp.float32)
    return y.astype(x.dtype)


def swiglu(x, w_gu, w_down):
    g, u = jnp.split(x @ w_gu, 2, axis=-1)
    return (jax.nn.silu(g) * u) @ w_down


def split_proj(p):
    offs = tuple(int(o) for o in np.cumsum(PROJ_SIZES)[:-1])
    return jnp.split(p, offs, axis=-1)


def alibi_slopes():
    h = jnp.arange(1, N_HEADS + 1, dtype=jnp.float32)
    return jnp.exp2(-8.0 * h / N_HEADS).reshape(N_KV_HEADS, GROUP)


def short_conv(u, prev, w):
    L = u.shape[1]
    u_pad = jnp.concatenate([prev, u], axis=1)
    y = sum(w[j] * u_pad[:, j:j + L] for j in range(CONV_W))
    return y, u_pad[:, -(CONV_W - 1):]


def indexer_scores(iq, iw, ik, qpos):
    s = jax.nn.relu(jnp.einsum('bqhd,bsd->bqhs', iq, ik))
    sc = jnp.einsum('bqhs,bqh->bqs', s, iw).astype(jnp.float32)
    kpos = jnp.arange(ik.shape[1])
    return jnp.where(kpos[None, None, :] <= qpos[None, :, None], sc, -jnp.inf)


def sparse_attend(q, k_sel, v_sel, sel, qpos):
    B, Q = q.shape[:2]
    qg = q.reshape(B, Q, N_KV_HEADS, GROUP, HEAD_DIM)
    s = jnp.einsum('bqhgd,bqnhd->bqhgn', qg, k_sel).astype(jnp.float32) * (HEAD_DIM ** -0.5)
    dist = qpos[None, :, None] - sel
    s = s - alibi_slopes()[None, None, :, :, None] * dist.astype(jnp.float32)[:, :, None, None, :]
    s = jnp.where((dist >= 0)[:, :, None, None, :], s, -jnp.inf)
    p = jax.nn.softmax(s, axis=-1).astype(v_sel.dtype)
    o = jnp.einsum('bqhgn,bqnhd->bqhgd', p, v_sel)
    return o.reshape(B, Q, D_ATTN)


gather_rows = jax.vmap(lambda a, i: a[i])


def prompt_attention(q, k, v, iq, iw, ik):
    B, T = q.shape[:2]
    n_sel = min(TOPK_MAX, T // 4)
    nb = -(-T // Q_BLOCK)
    T_pad = nb * Q_BLOCK
    pad = lambda a: jnp.pad(a, [(0, 0), (0, T_pad - T)] + [(0, 0)] * (a.ndim - 2))
    qp, iqp, iwp = pad(q), pad(iq), pad(iw)

    def block(i):
        start = i * Q_BLOCK
        qb = lax.dynamic_slice_in_dim(qp, start, Q_BLOCK, axis=1)
        iqb = lax.dynamic_slice_in_dim(iqp, start, Q_BLOCK, axis=1)
        iwb = lax.dynamic_slice_in_dim(iwp, start, Q_BLOCK, axis=1)
        qpos = start + jnp.arange(Q_BLOCK)
        sc = indexer_scores(iqb, iwb, ik, qpos)
        _, sel = lax.top_k(sc, n_sel)
        return sparse_attend(qb, gather_rows(k, sel), gather_rows(v, sel), sel, qpos)

    out = lax.map(block, jnp.arange(nb))
    return jnp.moveaxis(out, 0, 1).reshape(B, T_pad, D_ATTN)[:, :T]


def sample_attention(q, k, v, iq, iw, ik, cache_k, cache_v, cache_ik, page_table):
    DB, S = q.shape[:2]
    past = page_table.shape[1] * PAGE_SIZE
    n_sel = min(TOPK_MAX, (past + S) // 4)
    ik_past = cache_ik[page_table].reshape(DB, past, IDX_DIM)
    ik_all = jnp.concatenate([ik_past, ik], axis=1)
    qpos = past + jnp.arange(S)
    sc = indexer_scores(iq, iw, ik_all, qpos)
    _, sel = lax.top_k(sc, n_sel)
    in_past = (sel < past)[..., None, None]
    sp = jnp.minimum(sel, past - 1)
    phys = page_table[jnp.arange(DB)[:, None, None], sp // PAGE_SIZE]
    off = sp % PAGE_SIZE
    sn = jnp.clip(sel - past, 0, S - 1)
    k_sel = jnp.where(in_past, cache_k[phys, off], gather_rows(k, sn))
    v_sel = jnp.where(in_past, cache_v[phys, off], gather_rows(v, sn))
    return sparse_attend(q, k_sel, v_sel, sel, qpos)


def token_mix(x, conv_prev, attn_fn, lp):
    B, T = x.shape[:2]
    gb, gc, h, q, k, v, iq, ik, iw, g_conv, g_attn = split_proj(x @ lp['w_in'])
    cv, conv_state = short_conv(gc * h, conv_prev, lp['conv_w'])
    y_conv = (gb * cv) @ lp['w_conv_out']
    q = q.reshape(B, T, N_HEADS, HEAD_DIM)
    k = k.reshape(B, T, N_KV_HEADS, HEAD_DIM)
    v = v.reshape(B, T, N_KV_HEADS, HEAD_DIM)
    iq = iq.reshape(B, T, N_IDX_HEADS, IDX_DIM)
    iw = iw * ((N_IDX_HEADS * IDX_DIM) ** -0.5)
    y_attn = attn_fn(q, k, v, iq, iw, ik) @ lp['w_attn_out']
    merged = jax.nn.sigmoid(g_conv) * y_conv + jax.nn.sigmoid(g_attn) * y_attn
    return merged @ lp['w_o'], (k, v, ik, conv_state)


def layer(x, conv_prev, attn_fn, lp):
    x = layer_norm(ALPHA * x + 0.5 * swiglu(x, lp['ffn1_w_gu'], lp['ffn1_w_down']), lp['ln1_g'], lp['ln1_b'])
    mix, state = token_mix(x, conv_prev, attn_fn, lp)
    x = layer_norm(ALPHA * x + mix, lp['ln2_g'], lp['ln2_b'])
    x = layer_norm(ALPHA * x + 0.5 * swiglu(x, lp['ffn2_w_gu'], lp['ffn2_w_down']), lp['ln3_g'], lp['ln3_b'])
    return x, state


def setup_inputs(seed: int = 0) -> dict:
    key = jax.random.key(seed)
    ks = jax.random.split(key, 24)
    nrm = lambda k, shape: jax.random.normal(k, shape, jnp.float32)
    n_pages = PAST_LEN // PAGE_SIZE
    n_pool = (DEC_BATCH * n_pages * 5) // 4
    col_scale = jnp.concatenate([jnp.full((s,), BETA if i == V_COL_INDEX else 1.0, jnp.float32) for i, s in enumerate(PROJ_SIZES)])
    return {
        'x_prompt': nrm(ks[0], (BATCH, SEQ, D_MODEL)),
        'x_sample': nrm(ks[1], (DEC_BATCH, DEC_SEQ, D_MODEL)),
        'cache_k': nrm(ks[2], (DEPTH, n_pool, PAGE_SIZE, N_KV_HEADS, HEAD_DIM)),
        'cache_v': BETA * nrm(ks[3], (DEPTH, n_pool, PAGE_SIZE, N_KV_HEADS, HEAD_DIM)),
        'cache_idx_k': nrm(ks[4], (DEPTH, n_pool, PAGE_SIZE, IDX_DIM)),
        'state_conv': nrm(ks[5], (DEPTH, DEC_BATCH, CONV_W - 1, D_CONV)),
        'page_table': jax.random.permutation(ks[6], n_pool)[:DEC_BATCH * n_pages].reshape(DEC_BATCH, n_pages).astype(jnp.int32),
        'meta_tokens': nrm(ks[7], (N_META, D_MODEL)),
        'w_in': nrm(ks[8], (DEPTH, D_MODEL, D_IN)) * (D_MODEL ** -0.5) * col_scale,
        'conv_w': nrm(ks[9], (DEPTH, CONV_W, D_CONV)) * (CONV_W ** -0.5),
        'w_conv_out': nrm(ks[10], (DEPTH, D_CONV, D_MODEL)) * (D_CONV ** -0.5),
        'w_attn_out': nrm(ks[11], (DEPTH, D_ATTN, D_MODEL)) * (D_ATTN ** -0.5),
        'w_o': nrm(ks[12], (DEPTH, D_MODEL, D_MODEL)) * (D_MODEL ** -0.5) * BETA,
        'ffn1_w_gu': nrm(ks[13], (DEPTH, D_MODEL, 2 * D_FF)) * (D_MODEL ** -0.5),
        'ffn1_w_down': nrm(ks[14], (DEPTH, D_FF, D_MODEL)) * (D_FF ** -0.5) * BETA,
        'ffn2_w_gu': nrm(ks[15], (DEPTH, D_MODEL, 2 * D_FF)) * (D_MODEL ** -0.5),
        'ffn2_w_down': nrm(ks[16], (DEPTH, D_FF, D_MODEL)) * (D_FF ** -0.5) * BETA,
        'ln1_g': 1.0 + 0.02 * nrm(ks[17], (DEPTH, D_MODEL)),
        'ln1_b': 0.02 * nrm(ks[18], (DEPTH, D_MODEL)),
        'ln2_g': 1.0 + 0.02 * nrm(ks[19], (DEPTH, D_MODEL)),
        'ln2_b': 0.02 * nrm(ks[20], (DEPTH, D_MODEL)),
        'ln3_g': 1.0 + 0.02 * nrm(ks[21], (DEPTH, D_MODEL)),
        'ln3_b': 0.02 * nrm(ks[22], (DEPTH, D_MODEL)),
    }


def reference(x_prompt, x_sample, cache_k, cache_v, cache_idx_k, state_conv, page_table, meta_tokens,
              w_in, conv_w, w_conv_out, w_attn_out, w_o, ffn1_w_gu, ffn1_w_down, ffn2_w_gu, ffn2_w_down,
              ln1_g, ln1_b, ln2_g, ln2_b, ln3_g, ln3_b):
    B = x_prompt.shape[0]
    meta = jnp.broadcast_to(meta_tokens[None].astype(x_prompt.dtype), (B, N_META, D_MODEL))
    hp = jnp.concatenate([meta, x_prompt], axis=1)
    hs = x_sample
    kp_l, vp_l, ikp_l, cp_l, ks_l, vs_l, iks_l, cs_l = [], [], [], [], [], [], [], []
    for l in range(DEPTH):
        lp = {'w_in': w_in[l], 'conv_w': conv_w[l], 'w_conv_out': w_conv_out[l], 'w_attn_out': w_attn_out[l],
              'w_o': w_o[l], 'ffn1_w_gu': ffn1_w_gu[l], 'ffn1_w_down': ffn1_w_down[l],
              'ffn2_w_gu': ffn2_w_gu[l], 'ffn2_w_down': ffn2_w_down[l],
              'ln1_g': ln1_g[l], 'ln1_b': ln1_b[l], 'ln2_g': ln2_g[l], 'ln2_b': ln2_b[l],
              'ln3_g': ln3_g[l], 'ln3_b': ln3_b[l]}
        conv_zero = jnp.zeros((B, CONV_W - 1, D_CONV), hp.dtype)
        hp, (kp, vp, ikp, cp) = layer(hp, conv_zero, prompt_attention, lp)
        sample_fn = functools.partial(sample_attention, cache_k=cache_k[l], cache_v=cache_v[l],
                                      cache_ik=cache_idx_k[l], page_table=page_table)
        hs, (ks, vs, iks, cs) = layer(hs, state_conv[l], sample_fn, lp)
        kp_l.append(kp); vp_l.append(vp); ikp_l.append(ikp); cp_l.append(cp)
        ks_l.append(ks); vs_l.append(vs); iks_l.append(iks); cs_l.append(cs)
    y_prompt = hp[:, N_META:]
    y_sample = hs
    return (y_prompt, y_sample,
            jnp.stack(kp_l), jnp.stack(vp_l), jnp.stack(ikp_l), jnp.stack(cp_l),
            jnp.stack(ks_l), jnp.stack(vs_l), jnp.stack(iks_l), jnp.stack(cs_l))
```

```python
import functools

import numpy as np
import jax
import jax.numpy as jnp
from jax import lax
from jax.experimental import pallas as pl
from jax.experimental.pallas import tpu as pltpu

N_META = 16
N_HEADS = 16
HEAD_DIM = 128
N_KV_HEADS = 4
GROUP = N_HEADS // N_KV_HEADS
N_IDX_HEADS = 16
IDX_DIM = 64
TOPK_MAX = 256
CONV_W = 3
LN_EPS = 1e-5
PAGE_SIZE = 128

Q_BLOCK = 128
KEY_CHUNK = 256
NEG_BIG = -1e30
MIN_I32 = np.int32(-2 ** 31)
MIB = 1024 * 1024

F32 = jnp.float32
BF16 = jnp.bfloat16
NT_DIMS = (((1,), (1,)), ((), ()))


def _cparams(semantics, vmem_mib):
    return pltpu.CompilerParams(dimension_semantics=semantics, vmem_limit_bytes=int(vmem_mib * MIB))


def _layer_norm(y, g, b):
    mu = jnp.mean(y, axis=-1, keepdims=True)
    yc = y - mu
    var = jnp.mean(yc * yc, axis=-1, keepdims=True)
    return yc * lax.rsqrt(var + LN_EPS) * g + b


def _alibi_slope(head):
    return float(2.0 ** (-8.0 * (head + 1) / N_HEADS))


def _ffn_ln_kernel(x_ref, wg_ref, wu_ref, wd_ref, g_ref, b_ref, *rest, alpha, n_ff, emit_bf16):
    if emit_bf16:
        o_ref, ob_ref, xb_ref = rest
    else:
        o_ref, xb_ref = rest
    j = pl.program_id(1)

    @pl.when(j == 0)
    def _():
        xb_ref[...] = x_ref[...].astype(BF16)
        o_ref[...] = jnp.zeros_like(o_ref)

    xb = xb_ref[...]
    g = jnp.dot(xb, wg_ref[...], preferred_element_type=F32)
    u = jnp.dot(xb, wu_ref[...], preferred_element_type=F32)
    act = (g * jax.nn.sigmoid(g) * u).astype(BF16)
    o_ref[...] += jnp.dot(act, wd_ref[...], preferred_element_type=F32)

    @pl.when(j == n_ff - 1)
    def _():
        y = alpha * x_ref[...] + 0.5 * o_ref[...]
        out = _layer_norm(y, g_ref[...], b_ref[...])
        o_ref[...] = out
        if emit_bf16:
            ob_ref[...] = out.astype(BF16)


def _ffn_ln(x, w_gu, w_down, ln_g, ln_b, *, alpha, emit_bf16, tm=512, tf=512):
    rows, d = x.shape
    d_ff = w_down.shape[0]
    n_ff = d_ff // tf
    assert rows % tm == 0 and d_ff % tf == 0
    out_shape = [jax.ShapeDtypeStruct((rows, d), F32)]
    out_specs = [pl.BlockSpec((tm, d), lambda i, j: (i, 0))]
    if emit_bf16:
        out_shape.append(jax.ShapeDtypeStruct((rows, d), BF16))
        out_specs.append(pl.BlockSpec((tm, d), lambda i, j: (i, 0)))
    res = pl.pallas_call(
        functools.partial(_ffn_ln_kernel, alpha=alpha, n_ff=n_ff, emit_bf16=emit_bf16),
        grid=(rows // tm, n_ff),
        in_specs=[
            pl.BlockSpec((tm, d), lambda i, j: (i, 0)),
            pl.BlockSpec((d, tf), lambda i, j: (0, j)),
            pl.BlockSpec((d, tf), lambda i, j: (0, j + n_ff)),
            pl.BlockSpec((tf, d), lambda i, j: (j, 0)),
            pl.BlockSpec((1, d), lambda i, j: (0, 0)),
            pl.BlockSpec((1, d), lambda i, j: (0, 0)),
        ],
        out_specs=out_specs,
        out_shape=out_shape,
        scratch_shapes=[pltpu.VMEM((tm, d), BF16)],
        compiler_params=_cparams(("parallel", "arbitrary"), 52),
        name="ffn_ln",
    )(x, w_gu, w_gu, w_down, ln_g, ln_b)
    return res if emit_bf16 else res[0]


def _matmul_kernel(x_ref, w_ref, *o_refs):
    acc = jnp.dot(x_ref[...], w_ref[...], preferred_element_type=F32)
    for o_ref in o_refs:
        o_ref[...] = acc.astype(o_ref.dtype)


def _matmul(x, w, out_dtypes, *, tm, tn, name):
    rows, k = x.shape
    n = w.shape[1]
    assert rows % tm == 0 and n % tn == 0
    return pl.pallas_call(
        _matmul_kernel,
        grid=(n // tn, rows // tm),
        in_specs=[
            pl.BlockSpec((tm, k), lambda j, i: (i, 0)),
            pl.BlockSpec((k, tn), lambda j, i: (0, j)),
        ],
        out_specs=[pl.BlockSpec((tm, tn), lambda j, i: (i, j)) for _ in out_dtypes],
        out_shape=[jax.ShapeDtypeStruct((rows, n), dt) for dt in out_dtypes],
        compiler_params=_cparams(("parallel", "parallel"), 52),
        name=name,
    )(x, w)


def _shifted_rows(u, ubuf, tr):
    ubuf[8:8 + tr, :] = u
    return ubuf[7:7 + tr, :], ubuf[6:6 + tr, :]


def _conv_prompt_kernel(gb_ref, gc_ref, h_ref, w_ref, a_ref, st_ref, ubuf, *, tr, seq_pad, seq_len, n_seq):
    i = pl.program_id(1)
    u = gc_ref[...] * h_ref[...]

    @pl.when(i == 0)
    def _():
        ubuf[0:8, :] = jnp.zeros((8, ubuf.shape[1]), F32)

    @pl.when(i > 0)
    def _():
        ubuf[0:8, :] = ubuf[tr:tr + 8, :]

    u1, u2 = _shifted_rows(u, ubuf, tr)
    row = i * tr + lax.broadcasted_iota(jnp.int32, (tr, 1), 0)
    pos = row
    for _ in range(n_seq - 1):
        pos = jnp.where(pos >= seq_pad, pos - seq_pad, pos)
    w = w_ref[...]
    cv = w[2:3] * u + w[1:2] * jnp.where(pos >= 1, u1, 0.0) + w[0:1] * jnp.where(pos >= 2, u2, 0.0)
    a_ref[...] = (gb_ref[...] * cv).astype(BF16)

    for b in range(n_seq):
        r0 = b * seq_pad + seq_len - (CONV_W - 1)
        tile, loc = divmod(r0, tr)
        assert loc + (CONV_W - 1) <= tr

        @pl.when(i == tile)
        def _(b=b, loc=loc):
            st_ref[2 * b:2 * b + 2, :] = u[loc:loc + 2, :]


def _conv_sample_kernel(gb_ref, gc_ref, h_ref, w_ref, s1_ref, s2_ref, a_ref, u_ref, ubuf, *, tr, dec_seq):
    u = gc_ref[...] * h_ref[...]
    ubuf[0:8, :] = jnp.zeros((8, ubuf.shape[1]), F32)
    u1, u2 = _shifted_rows(u, ubuf, tr)
    t = lax.broadcasted_iota(jnp.int32, (tr, 1), 0) % dec_seq
    w = w_ref[...]
    cv = (w[2:3] * u + w[1:2] * (jnp.where(t >= 1, u1, 0.0) + s1_ref[...])
          + w[0:1] * (jnp.where(t >= 2, u2, 0.0) + s2_ref[...]))
    a_ref[...] = (gb_ref[...] * cv).astype(BF16)
    u_ref[...] = u


def _sortable_key(x):
    bits = pltpu.bitcast(x, jnp.int32)
    return jnp.where(bits < 0, bits ^ jnp.int32(0x7FFFFFFF), bits)


def _kth_largest_key(load_chunk, n_chunks, rows, width, k):
    def bit_body(it, t):
        cand = t | lax.shift_left(jnp.int32(1), 31 - it)
        cand_s = cand ^ MIN_I32

        def chunk_body(c, cnt):
            return cnt + jnp.where(load_chunk(c) >= cand_s, 1.0, 0.0)

        cnt = lax.fori_loop(0, n_chunks, chunk_body, jnp.zeros((rows, width), F32))
        tot = jnp.sum(cnt, axis=1, keepdims=True)
        return jnp.where(tot >= k, cand, t)

    t = lax.fori_loop(0, 32, bit_body, jnp.zeros((rows, 1), jnp.int32))
    return t ^ MIN_I32


def _prompt_attn_kernel(q_ref, iq_ref, iw_ref, k_ref, v_ref, ika_ref, ikb_ref, o_ref, key_ref, nb_ref,
                        *, nq_real, n_sel):
    i = pl.program_id(1)
    ck = KEY_CHUNK

    @pl.when(i >= nq_real)
    def _():
        o_ref[...] = jnp.zeros_like(o_ref)

    @pl.when(i < nq_real)
    def _():
        n_ck = (i * Q_BLOCK + Q_BLOCK + ck - 1) // ck
        qpos = i * Q_BLOCK + lax.broadcasted_iota(jnp.int32, (Q_BLOCK, 1), 0)

        iq = iq_ref[...]
        n_pair = N_IDX_HEADS // 2
        lhs = jnp.concatenate([iq[:, 128 * j:128 * (j + 1)] for j in range(n_pair)], axis=0)
        iw = iw_ref[...] * float((N_IDX_HEADS * IDX_DIM) ** -0.5)
        wcol = [iw[:, h:h + 1] for h in range(N_IDX_HEADS)]

        def idx_body(c, carry):
            k0 = pl.multiple_of(c * ck, ck)
            sa = lax.dot_general(lhs, ika_ref[pl.ds(k0, ck), :], NT_DIMS, preferred_element_type=F32)
            sb = lax.dot_general(lhs, ikb_ref[pl.ds(k0, ck), :], NT_DIMS, preferred_element_type=F32)
            acc = jnp.zeros((Q_BLOCK, ck), F32)
            for j in range(n_pair):
                rows = slice(j * Q_BLOCK, (j + 1) * Q_BLOCK)
                acc = acc + jnp.maximum(sa[rows], 0.0) * wcol[2 * j]
                acc = acc + jnp.maximum(sb[rows], 0.0) * wcol[2 * j + 1]
            kpos = k0 + lax.broadcasted_iota(jnp.int32, (1, ck), 1)
            sc = jnp.where(kpos <= qpos, acc, -jnp.inf)
            key_ref[:, pl.ds(k0, ck)] = _sortable_key(sc)
            return carry

        lax.fori_loop(0, n_ck, idx_body, 0)

        thr = _kth_largest_key(lambda c: key_ref[:, pl.ds(pl.multiple_of(c * ck, ck), ck)],
                               n_ck, Q_BLOCK, ck, n_sel)

        def mask_body(c, carry):
            k0 = pl.multiple_of(c * ck, ck)
            kpos = k0 + lax.broadcasted_iota(jnp.int32, (1, ck), 1)
            sel = jnp.where(key_ref[:, pl.ds(k0, ck)] >= thr, 0.0, NEG_BIG)
            nb_ref[:, pl.ds(k0, ck)] = jnp.where(kpos <= qpos, sel, NEG_BIG)
            return carry

        lax.fori_loop(0, n_ck, mask_body, 0)

        scale = float(HEAD_DIM ** -0.5)
        qposf = qpos.astype(F32)
        for kv in range(N_KV_HEADS):
            heads = [kv * GROUP + g for g in range(GROUP)]
            qg = jnp.concatenate([q_ref[:, HEAD_DIM * h:HEAD_DIM * (h + 1)] for h in heads], axis=0)

            def att_body(c, carry, kv=kv, heads=heads, qg=qg):
                m, l, acc = carry
                k0 = pl.multiple_of(c * ck, ck)
                kc = k_ref[pl.ds(k0, ck), HEAD_DIM * kv:HEAD_DIM * (kv + 1)]
                vc = v_ref[pl.ds(k0, ck), HEAD_DIM * kv:HEAD_DIM * (kv + 1)]
                s = lax.dot_general(qg, kc, NT_DIMS, preferred_element_type=F32)
                kposf = (k0 + lax.broadcasted_iota(jnp.int32, (1, ck), 1)).astype(F32)
                dist = qposf - kposf
                nb = nb_ref[:, pl.ds(k0, ck)]
                z = jnp.concatenate(
                    [s[g * Q_BLOCK:(g + 1) * Q_BLOCK] * scale - _alibi_slope(h) * dist + nb
                     for g, h in enumerate(heads)], axis=0)
                m_new = jnp.maximum(m, jnp.max(z, axis=1, keepdims=True))
                a = jnp.exp(m - m_new)
                p = jnp.exp(z - m_new)
                l_new = a * l + jnp.sum(p, axis=1, keepdims=True)
                acc_new = a * acc + jnp.dot(p.astype(BF16), vc, preferred_element_type=F32)
                return m_new, l_new, acc_new

            rows = GROUP * Q_BLOCK
            m, l, acc = lax.fori_loop(
                0, n_ck, att_body,
                (jnp.full((rows, 1), NEG_BIG, F32), jnp.zeros((rows, 1), F32), jnp.zeros((rows, HEAD_DIM), F32)))
            out = acc / l
            for g, h in enumerate(heads):
                o_ref[:, HEAD_DIM * h:HEAD_DIM * (h + 1)] = out[g * Q_BLOCK:(g + 1) * Q_BLOCK].astype(BF16)


def _sample_idx_kernel(pt_ref, iq_ref, iw_ref, iknew_ref, *rest, n_pages, dec_seq):
    page_refs, sc_ref = rest[:n_pages], rest[n_pages]
    past = n_pages * PAGE_SIZE
    lhs = iq_ref[0]
    w = iw_ref[0] * float((N_IDX_HEADS * IDX_DIM) ** -0.5)
    ik = jnp.concatenate([r[...].astype(BF16) for r in page_refs] + [iknew_ref[0]], axis=0)
    s = lax.dot_general(lhs, ik, NT_DIMS, preferred_element_type=F32)
    s = jnp.maximum(s, 0.0) * w
    width = s.shape[1]
    sc = jnp.sum(s.reshape(dec_seq, N_IDX_HEADS, width), axis=1)
    t = lax.broadcasted_iota(jnp.int32, (dec_seq, 1), 0)
    col = lax.broadcasted_iota(jnp.int32, (1, width), 1)
    sc_ref[0] = jnp.where(col - past <= t, sc, -jnp.inf)


def _sample_mask_kernel(sc_ref, nb_ref, key_ref, *, n_sel, past, dec_seq):
    rows, width = sc_ref.shape
    cw = 128
    key_ref[...] = _sortable_key(sc_ref[...])
    thr = _kth_largest_key(lambda c: key_ref[:, pl.ds(pl.multiple_of(c * cw, cw), cw)],
                           width // cw, rows, cw, n_sel)
    t = lax.broadcasted_iota(jnp.int32, (rows, 1), 0) % dec_seq
    col = lax.broadcasted_iota(jnp.int32, (1, width), 1)
    sel = jnp.where(key_ref[...] >= thr, 0.0, NEG_BIG)
    nb_ref[...] = jnp.where(col - past <= t, sel, NEG_BIG)


def _sample_attn_kernel(pt_ref, q_ref, nb_ref, knew_ref, vnew_ref, *rest, n_pages, dec_seq):
    k_pages, v_pages, o_ref = rest[:n_pages], rest[n_pages:2 * n_pages], rest[2 * n_pages]
    past = n_pages * PAGE_SIZE
    nb = nb_ref[0]
    width = nb.shape[1]
    qpos = past + lax.broadcasted_iota(jnp.int32, (dec_seq, 1), 0)
    kpos = lax.broadcasted_iota(jnp.int32, (1, width), 1)
    dist = (qpos - kpos).astype(F32)
    scale = float(HEAD_DIM ** -0.5)
    rows = GROUP * dec_seq
    for kv in range(N_KV_HEADS):
        def gather(pages, new_ref):
            old = [r[pl.ds(kv, PAGE_SIZE, stride=N_KV_HEADS), :].astype(BF16) for r in pages]
            return jnp.concatenate(old + [new_ref[0, :, HEAD_DIM * kv:HEAD_DIM * (kv + 1)]], axis=0)

        kk = gather(k_pages, knew_ref)
        vv = gather(v_pages, vnew_ref)
        qk = q_ref[0, rows * kv:rows * (kv + 1), :]
        s = lax.dot_general(qk, kk, NT_DIMS, preferred_element_type=F32)
        z = jnp.concatenate(
            [s[g * dec_seq:(g + 1) * dec_seq] * scale - _alibi_slope(kv * GROUP + g) * dist + nb
             for g in range(GROUP)], axis=0)
        m = jnp.max(z, axis=1, keepdims=True)
        p = jnp.exp(z - m)
        l = jnp.sum(p, axis=1, keepdims=True)
        o = jnp.dot(p.astype(BF16), vv, preferred_element_type=F32) / l
        o_ref[0, rows * kv:rows * (kv + 1), :] = o.astype(BF16)


def _mix_ln_kernel(a_ref, b_ref, gcv_ref, gat_ref, x_ref, wc_ref, wa_ref, wo_ref, g_ref, bb_ref, o_ref, *, alpha):
    y_conv = jnp.dot(a_ref[...], wc_ref[...], preferred_element_type=F32)
    y_attn = jnp.dot(b_ref[...], wa_ref[...], preferred_element_type=F32)
    merged = jax.nn.sigmoid(gcv_ref[...]) * y_conv + jax.nn.sigmoid(gat_ref[...]) * y_attn
    mix = jnp.dot(merged.astype(BF16), wo_ref[...], preferred_element_type=F32)
    o_ref[...] = _layer_norm(alpha * x_ref[...] + mix, g_ref[...], bb_ref[...])


def _layer(x, lp, cache_k, cache_v, cache_ik, state_conv, page_table, *, alpha, n_seq, seq_pad, seq_len,
           dec_batch, dec_seq):
    rows, d = x.shape
    rows_p = n_seq * seq_pad
    rows_s = dec_batch * dec_seq
    assert rows == rows_p + rows_s
    n_pages = page_table.shape[1]
    past = n_pages * PAGE_SIZE
    d_kv = N_KV_HEADS * HEAD_DIM
    d_iq = N_IDX_HEADS * IDX_DIM

    sizes = (d, d, d, N_HEADS * HEAD_DIM, d_kv, d_kv, d_iq, IDX_DIM, N_IDX_HEADS, d, d)
    offs = np.concatenate([[0], np.cumsum(sizes)])
    w_in = lp['w_in']
    assert w_in.shape[1] == offs[-1]
    col = lambda idx: w_in[:, offs[idx]:offs[idx + 1]]
    gb_w, gc_w, h_w, q_w, k_w, v_w, iq_w, ik_w, iw_w, gcv_w, gat_w = [col(n) for n in range(11)]
    z64 = jnp.zeros((d, 128 - IDX_DIM), w_in.dtype)
    w1 = jnp.concatenate([gb_w, gc_w, h_w, gcv_w, gat_w], axis=1).astype(BF16)
    w2 = jnp.concatenate([q_w, iq_w], axis=1).astype(BF16)
    w3 = jnp.concatenate([k_w, v_w, ik_w, z64, z64, ik_w, iw_w,
                          jnp.zeros((d, 128 - N_IDX_HEADS), w_in.dtype)], axis=1).astype(BF16)
    n3 = w3.shape[1]

    x1, x1b = _ffn_ln(x, lp['ffn1_w_gu'].astype(BF16), lp['ffn1_w_down'].astype(BF16),
                      lp['ln1_g'][None], lp['ln1_b'][None], alpha=alpha, emit_bf16=True)

    (p1,) = _matmul(x1b, w1, [F32], tm=1024, tn=1024, name="proj_f32")
    (p2,) = _matmul(x1b, w2, [BF16], tm=1024, tn=1024, name="proj_bf16")
    p3f, p3b = _matmul(x1b, w3, [F32, BF16], tm=1024, tn=n3, name="proj_kv")
    c_k, c_v, c_ika, c_ikb, c_iw = 0, d_kv, 2 * d_kv, 2 * d_kv + 128, 2 * d_kv + 256

    tr, tc = 512, 1024
    ncb = d // tc
    conv_w = lp['conv_w']
    a_p, st_p = pl.pallas_call(
        functools.partial(_conv_prompt_kernel, tr=tr, seq_pad=seq_pad, seq_len=seq_len, n_seq=n_seq),
        grid=(ncb, rows_p // tr),
        in_specs=[
            pl.BlockSpec((tr, tc), lambda j, i: (i, j)),
            pl.BlockSpec((tr, tc), lambda j, i: (i, ncb + j)),
            pl.BlockSpec((tr, tc), lambda j, i: (i, 2 * ncb + j)),
            pl.BlockSpec((CONV_W, tc), lambda j, i: (0, j)),
        ],
        out_specs=[
            pl.BlockSpec((tr, tc), lambda j, i: (i, j)),
            pl.BlockSpec((n_seq * (CONV_W - 1), tc), lambda j, i: (0, j)),
        ],
        out_shape=[jax.ShapeDtypeStruct((rows_p, d), BF16),
                   jax.ShapeDtypeStruct((n_seq * (CONV_W - 1), d), F32)],
        scratch_shapes=[pltpu.VMEM((tr + 8, tc), F32)],
        compiler_params=_cparams(("parallel", "arbitrary"), 40),
        name="conv_prompt",
    )(p1, p1, p1, conv_w)

    assert rows_s == tr and rows_p % tr == 0
    sb = rows_p // tr
    zero = jnp.zeros((dec_batch, 1, d), F32)
    s1 = jnp.concatenate([state_conv[:, 1:2]] + [zero] * (dec_seq - 1), axis=1).reshape(rows_s, d)
    s2 = jnp.concatenate([state_conv[:, 0:1], state_conv[:, 1:2]] + [zero] * (dec_seq - 2), axis=1).reshape(rows_s, d)
    a_s, u_s = pl.pallas_call(
        functools.partial(_conv_sample_kernel, tr=tr, dec_seq=dec_seq),
        grid=(ncb,),
        in_specs=[
            pl.BlockSpec((tr, tc), lambda j: (sb, j)),
            pl.BlockSpec((tr, tc), lambda j: (sb, ncb + j)),
            pl.BlockSpec((tr, tc), lambda j: (sb, 2 * ncb + j)),
            pl.BlockSpec((CONV_W, tc), lambda j: (0, j)),
            pl.BlockSpec((tr, tc), lambda j: (0, j)),
            pl.BlockSpec((tr, tc), lambda j: (0, j)),
        ],
        out_specs=[pl.BlockSpec((tr, tc), lambda j: (0, j)), pl.BlockSpec((tr, tc), lambda j: (0, j))],
        out_shape=[jax.ShapeDtypeStruct((rows_s, d), BF16), jax.ShapeDtypeStruct((rows_s, d), F32)],
        scratch_shapes=[pltpu.VMEM((tr + 8, tc), F32)],
        compiler_params=_cparams(("parallel",), 40),
        name="conv_sample",
    )(p1, p1, p1, conv_w, s1, s2)

    nqb = seq_pad // Q_BLOCK
    nq_real = -(-seq_len // Q_BLOCK)
    n_sel_p = min(TOPK_MAX, seq_len // 4)
    assert seq_pad % KEY_CHUNK == 0 and n_sel_p <= KEY_CHUNK
    qcols = N_HEADS * HEAD_DIM
    b_p = pl.pallas_call(
        functools.partial(_prompt_attn_kernel, nq_real=nq_real, n_sel=n_sel_p),
        grid=(n_seq, nqb),
        in_specs=[
            pl.BlockSpec((Q_BLOCK, qcols), lambda b, i: (b * nqb + i, 0)),
            pl.BlockSpec((Q_BLOCK, d_iq), lambda b, i: (b * nqb + i, qcols // d_iq)),
            pl.BlockSpec((Q_BLOCK, 128), lambda b, i: (b * nqb + i, c_iw // 128)),
            pl.BlockSpec((seq_pad, d_kv), lambda b, i: (b, c_k // d_kv)),
            pl.BlockSpec((seq_pad, d_kv), lambda b, i: (b, c_v // d_kv)),
            pl.BlockSpec((seq_pad, 128), lambda b, i: (b, c_ika // 128)),
            pl.BlockSpec((seq_pad, 128), lambda b, i: (b, c_ikb // 128)),
        ],
        out_specs=pl.BlockSpec((Q_BLOCK, qcols), lambda b, i: (b * nqb + i, 0)),
        out_shape=jax.ShapeDtypeStruct((rows_p, qcols), BF16),
        scratch_shapes=[pltpu.VMEM((Q_BLOCK, seq_pad), jnp.int32), pltpu.VMEM((Q_BLOCK, seq_pad), F32)],
        compiler_params=_cparams(("parallel", "arbitrary"), 48),
        name="prompt_attn",
    )(p2, p2, p3f, p3b, p3b, p3b, p3b)

    n_sel_s = min(TOPK_MAX, (past + dec_seq) // 4)
    new_pad = 128
    width = past + new_pad
    pt_flat = page_table.reshape(-1)
    q_s = p2[rows_p:, :qcols].reshape(dec_batch, dec_seq, N_KV_HEADS, GROUP, HEAD_DIM)
    q_s = q_s.transpose(0, 2, 3, 1, 4).reshape(dec_batch, N_HEADS * dec_seq, HEAD_DIM)
    iq_s = p2[rows_p:, qcols:].reshape(dec_batch, dec_seq * N_IDX_HEADS, IDX_DIM)
    iw_s = p3f[rows_p:, c_iw:c_iw + N_IDX_HEADS].reshape(dec_batch, dec_seq * N_IDX_HEADS, 1)

    def new_page(cols):
        rows_new = p3b[rows_p:, cols].reshape(dec_batch, dec_seq, -1)
        return jnp.pad(rows_new, ((0, 0), (0, new_pad - dec_seq), (0, 0)))

    ik_new = new_page(slice(c_ika, c_ika + IDX_DIM))
    k_new = new_page(slice(c_k, c_k + d_kv))
    v_new = new_page(slice(c_v, c_v + d_kv))
    ik_cache = cache_ik.reshape(-1, IDX_DIM)
    k_cache = cache_k.reshape(-1, HEAD_DIM)
    v_cache = cache_v.reshape(-1, HEAD_DIM)

    def page_spec(shape, p):
        return pl.BlockSpec(shape, lambda s, pt, p=p: (pt[s * n_pages + p], 0))

    per_seq = lambda shape: pl.BlockSpec((1,) + shape, lambda s, pt: (s, 0, 0))
    sc_s = pl.pallas_call(
        functools.partial(_sample_idx_kernel, n_pages=n_pages, dec_seq=dec_seq),
        grid_spec=pltpu.PrefetchScalarGridSpec(
            num_scalar_prefetch=1,
            grid=(dec_batch,),
            in_specs=[per_seq((dec_seq * N_IDX_HEADS, IDX_DIM)), per_seq((dec_seq * N_IDX_HEADS, 1)),
                      per_seq((new_pad, IDX_DIM))]
                     + [page_spec((PAGE_SIZE, IDX_DIM), p) for p in range(n_pages)],
            out_specs=per_seq((dec_seq, width)),
        ),
        out_shape=jax.ShapeDtypeStruct((dec_batch, dec_seq, width), F32),
        compiler_params=_cparams(("parallel",), 32),
        name="sample_indexer",
    )(pt_flat, iq_s, iw_s, ik_new, *([ik_cache] * n_pages))

    mrows = 128
    nb_s = pl.pallas_call(
        functools.partial(_sample_mask_kernel, n_sel=n_sel_s, past=past, dec_seq=dec_seq),
        grid=(rows_s // mrows,),
        in_specs=[pl.BlockSpec((mrows, width), lambda r: (r, 0))],
        out_specs=pl.BlockSpec((mrows, width), lambda r: (r, 0)),
        out_shape=jax.ShapeDtypeStruct((rows_s, width), F32),
        scratch_shapes=[pltpu.VMEM((mrows, width), jnp.int32)],
        compiler_params=_cparams(("parallel",), 32),
        name="sample_mask",
    )(sc_s.reshape(rows_s, width))

    kv_rows = PAGE_SIZE * N_KV_HEADS
    b_s = pl.pallas_call(
        functools.partial(_sample_attn_kernel, n_pages=n_pages, dec_seq=dec_seq),
        grid_spec=pltpu.PrefetchScalarGridSpec(
            num_scalar_prefetch=1,
            grid=(dec_batch,),
            in_specs=[per_seq((N_HEADS * dec_seq, HEAD_DIM)), per_seq((dec_seq, width)),
                      per_seq((new_pad, d_kv)), per_seq((new_pad, d_kv))]
                     + [page_spec((kv_rows, HEAD_DIM), p) for p in range(n_pages)] * 2,
            out_specs=per_seq((N_HEADS * dec_seq, HEAD_DIM)),
        ),
        out_shape=jax.ShapeDtypeStruct((dec_batch, N_HEADS * dec_seq, HEAD_DIM), BF16),
        compiler_params=_cparams(("parallel",), 48),
        name="sample_attn",
    )(pt_flat, q_s, nb_s.reshape(dec_batch, dec_seq, width), k_new, v_new,
      *([k_cache] * n_pages), *([v_cache] * n_pages))
    b_s = b_s.reshape(dec_batch, N_KV_HEADS, GROUP, dec_seq, HEAD_DIM).transpose(0, 3, 1, 2, 4)
    b_s = b_s.reshape(rows_s, qcols)

    a_all = jnp.concatenate([a_p, a_s], axis=0)
    b_all = jnp.concatenate([b_p, b_s], axis=0)
    tm = 256
    whole = lambda shape: pl.BlockSpec(shape, lambda i: (0, 0), pipeline_mode=pl.Buffered(1))
    row_blk = lambda cb: pl.BlockSpec((tm, d), lambda i, cb=cb: (i, cb))
    x2 = pl.pallas_call(
        functools.partial(_mix_ln_kernel, alpha=alpha),
        grid=(rows // tm,),
        in_specs=[row_blk(0), row_blk(0), row_blk(3), row_blk(4), row_blk(0),
                  whole((d, d)), whole((qcols, d)), whole((d, d)), whole((1, d)), whole((1, d))],
        out_specs=row_blk(0),
        out_shape=jax.ShapeDtypeStruct((rows, d), F32),
        compiler_params=_cparams(("parallel",), 52),
        name="mix_ln",
    )(a_all, b_all, p1, p1, x1, lp['w_conv_out'].astype(BF16), lp['w_attn_out'].astype(BF16),
      lp['w_o'].astype(BF16), lp['ln2_g'][None], lp['ln2_b'][None])

    x3 = _ffn_ln(x2, lp['ffn2_w_gu'].astype(BF16), lp['ffn2_w_down'].astype(BF16),
                 lp['ln3_g'][None], lp['ln3_b'][None], alpha=alpha, emit_bf16=False)

    def prompt_rows(arr, cols, tail):
        blk = arr[:rows_p, cols].reshape((n_seq, seq_pad) + tail)
        return blk[:, :seq_len]

    state = (
        prompt_rows(p3f, slice(c_k, c_k + d_kv), (N_KV_HEADS, HEAD_DIM)),
        prompt_rows(p3f, slice(c_v, c_v + d_kv), (N_KV_HEADS, HEAD_DIM)),
        prompt_rows(p3f, slice(c_ika, c_ika + IDX_DIM), (IDX_DIM,)),
        st_p.reshape(n_seq, CONV_W - 1, d),
        p3f[rows_p:, c_k:c_k + d_kv].reshape(dec_batch, dec_seq, N_KV_HEADS, HEAD_DIM),
        p3f[rows_p:, c_v:c_v + d_kv].reshape(dec_batch, dec_seq, N_KV_HEADS, HEAD_DIM),
        p3f[rows_p:, c_ika:c_ika + IDX_DIM].reshape(dec_batch, dec_seq, IDX_DIM),
        u_s.reshape(dec_batch, dec_seq, d)[:, dec_seq - (CONV_W - 1):],
    )
    return x3, state


def kernel(x_prompt, x_sample, cache_k, cache_v, cache_idx_k, state_conv, page_table, meta_tokens, w_in, conv_w, w_conv_out, w_attn_out, w_o, ffn1_w_gu, ffn1_w_down, ffn2_w_gu, ffn2_w_down, ln1_g, ln1_b, ln2_g, ln2_b, ln3_g, ln3_b):
    n_seq, seq, d = x_prompt.shape
    dec_batch, dec_seq, _ = x_sample.shape
    depth = w_in.shape[0]
    seq_len = seq + N_META
    seq_pad = -(-seq_len // KEY_CHUNK) * KEY_CHUNK
    alpha = float((2.0 * depth) ** 0.25)

    pad = jnp.zeros((seq_pad - seq_len, d), x_prompt.dtype)
    pieces = []
    for b in range(n_seq):
        pieces += [meta_tokens.astype(x_prompt.dtype), x_prompt[b], pad]
    pieces.append(x_sample.reshape(dec_batch * dec_seq, d))
    x = jnp.concatenate(pieces, axis=0)

    states = []
    for l in range(depth):
        lp = {'w_in': w_in[l], 'conv_w': conv_w[l], 'w_conv_out': w_conv_out[l], 'w_attn_out': w_attn_out[l],
              'w_o': w_o[l], 'ffn1_w_gu': ffn1_w_gu[l], 'ffn1_w_down': ffn1_w_down[l],
              'ffn2_w_gu': ffn2_w_gu[l], 'ffn2_w_down': ffn2_w_down[l],
              'ln1_g': ln1_g[l], 'ln1_b': ln1_b[l], 'ln2_g': ln2_g[l], 'ln2_b': ln2_b[l],
              'ln3_g': ln3_g[l], 'ln3_b': ln3_b[l]}
        x, st = _layer(x, lp, cache_k[l], cache_v[l], cache_idx_k[l], state_conv[l], page_table,
                       alpha=alpha, n_seq=n_seq, seq_pad=seq_pad, seq_len=seq_len,
                       dec_batch=dec_batch, dec_seq=dec_seq)
        states.append(st)

    rows_p = n_seq * seq_pad
    y_prompt = x[:rows_p].reshape(n_seq, seq_pad, d)[:, N_META:seq_len]
    y_sample = x[rows_p:].reshape(dec_batch, dec_seq, d)
    stacked = [jnp.stack([st[n] for st in states]) for n in range(8)]
    return (y_prompt, y_sample, *stacked)
```
